```python
import jax, jax.numpy as jnp
from jax import lax
import numpy as np

D_MODEL = 2048
BATCH = 8
SEQ = 2048
DEPTH = 2

N_MIXERS = 2
N_MLSTM_LAYERS = (DEPTH + 1) // 2
N_ATTN_LAYERS = DEPTH // 2
N_MOD = 6

M_HEADS = 8
M_DV = D_MODEL // M_HEADS
M_DQK = M_DV // 2
M_CHUNK = 64
M_GATE_CAP = 15.0
M_QK_W = M_HEADS * M_DQK
M_V_W = M_HEADS * M_DV
M_IN_W = 2 * M_QK_W + 2 * M_V_W + 2 * M_HEADS

A_HEAD_DIM = 128
A_HEADS = D_MODEL // A_HEAD_DIM
A_GROUPS = ((128, 1), (512, 4), (2048, 16))
A_N_GROUPS = len(A_GROUPS)
A_BLOCK = 128
A_WIDTH = A_HEADS * A_HEAD_DIM
A_IN_W = A_N_GROUPS * 3 * A_WIDTH
ROPE_THETA = 500000.0
ROPE_DIM = A_HEAD_DIM // 4

D_FF = 5632
CONV_WIDTH = 3
EPS = 1e-6

kernel_name = "hybrid_mlstm_dilated_attn_convffn_adaln"


def rms_norm(x, g):
    xf = x.astype(jnp.float32)
    y = xf * lax.rsqrt(jnp.mean(xf * xf, axis=-1, keepdims=True) + EPS)
    return (y * g.astype(jnp.float32)).astype(x.dtype)


def partial_rope(t, positions):
    half = ROPE_DIM // 2
    inv_freq = ROPE_THETA ** (-jnp.arange(half, dtype=jnp.float32) / half)
    ang = positions.astype(jnp.float32)[..., None, None] * inv_freq
    cos, sin = jnp.cos(ang), jnp.sin(ang)
    t1, t2, rest = t[..., :half], t[..., half:ROPE_DIM], t[..., ROPE_DIM:]
    return jnp.concatenate([t1 * cos - t2 * sin, t2 * cos + t1 * sin, rest], axis=-1)


def _chunk_major(t, n_chunks):
    B, S, H = t.shape[:3]
    t = t.reshape((B, n_chunks, M_CHUNK, H) + t.shape[3:])
    return t.transpose((1, 0, 3, 2) + tuple(range(4, t.ndim)))


def mlstm_chunkwise(q, k, v, log_i, log_f):
    B, S, H, _ = q.shape
    n_chunks = S // M_CHUNK
    xs = tuple(_chunk_major(t, n_chunks) for t in (q, k, v, log_i, log_f))
    causal = jnp.tril(jnp.ones((M_CHUNK, M_CHUNK), dtype=bool))

    def step(carry, inp):
        C, n, m = carry
        qc, kc, vc, ic, fc = inp
        b = jnp.cumsum(fc, axis=-1)
        dmat = b[..., :, None] - b[..., None, :] + ic[..., None, :]
        dmat = jnp.where(causal, dmat, -jnp.inf)
        inter = b + m[..., None]
        m_s = jnp.maximum(inter, jnp.max(dmat, axis=-1))
        a_inter = jnp.exp(inter - m_s)
        w_intra = jnp.einsum('bhsd,bhud->bhsu', qc, kc) * jnp.exp(dmat - m_s[..., None])
        num = (jnp.einsum('bhsu,bhue->bhse', w_intra, vc)
               + a_inter[..., None] * jnp.einsum('bhsd,bhed->bhse', qc, C))
        den = jnp.sum(w_intra, axis=-1) + a_inter * jnp.einsum('bhsd,bhd->bhs', qc, n)
        hc = num / jnp.maximum(jnp.abs(den), jnp.exp(-m_s))[..., None]
        b_last = b[..., -1]
        g = b_last[..., None] - b + ic
        m_new = jnp.maximum(b_last + m, jnp.max(g, axis=-1))
        wk = jnp.exp(g - m_new[..., None])
        decay = jnp.exp(b_last + m - m_new)
        C_new = decay[..., None, None] * C + jnp.einsum('bhu,bhue,bhud->bhed', wk, vc, kc)
        n_new = decay[..., None] * n + jnp.einsum('bhu,bhud->bhd', wk, kc)
        return (C_new, n_new, m_new), hc

    init = (jnp.zeros((B, H, M_DV, M_DQK), jnp.float32),
            jnp.zeros((B, H, M_DQK), jnp.float32),
            jnp.zeros((B, H), jnp.float32))
    _, hs = lax.scan(step, init, xs)
    return hs.transpose(1, 0, 3, 2, 4).reshape(B, S, H, M_DV)


def mlstm_mixer(h, w_in, gate_b, head_g, w_out):
    B, S, _ = h.shape
    proj = (h @ w_in).astype(jnp.float32)
    cuts = [M_QK_W, 2 * M_QK_W, 2 * M_QK_W + M_V_W, 2 * M_QK_W + 2 * M_V_W,
            2 * M_QK_W + 2 * M_V_W + M_HEADS]
    q, k, v, o, ig, fg = jnp.split(proj, cuts, axis=-1)
    q = q.reshape(B, S, M_HEADS, M_DQK) * (M_DQK ** -0.5)
    k = k.reshape(B, S, M_HEADS, M_DQK)
    v = v.reshape(B, S, M_HEADS, M_DV)
    gb = gate_b.astype(jnp.float32)
    log_i = M_GATE_CAP * jnp.tanh((ig + gb[:M_HEADS]) / M_GATE_CAP)
    log_f = jax.nn.log_sigmoid(M_GATE_CAP * jnp.tanh((fg + gb[M_HEADS:]) / M_GATE_CAP))
    ht = mlstm_chunkwise(q, k, v, log_i, log_f)
    ht = ht * lax.rsqrt(jnp.mean(ht * ht, axis=-1, keepdims=True) + EPS)
    ht = ht * head_g.astype(jnp.float32).reshape(M_HEADS, M_DV)
    y = jax.nn.sigmoid(o) * ht.reshape(B, S, M_V_W)
    return y.astype(h.dtype) @ w_out


def dilated_band_attention(q, k, v, window, dilation):
    B, S, H, dh = q.shape
    L = S // dilation
    span = window // dilation
    Q = A_BLOCK
    nb = -(-L // Q)
    pad = nb * Q - L

    def to_sub(t):
        return t.reshape(B, L, dilation, H, dh).transpose(0, 2, 1, 3, 4)

    qs = jnp.pad(to_sub(q), ((0, 0), (0, 0), (0, pad), (0, 0), (0, 0))).reshape(B, dilation, nb, Q, H, dh)

    def key_blocks(t):
        tp = jnp.pad(to_sub(t), ((0, 0), (0, 0), (Q, pad), (0, 0), (0, 0))).reshape(B, dilation, nb + 1, Q, H, dh)
        return jnp.concatenate([tp[:, :, :-1], tp[:, :, 1:]], axis=3)

    ks, vs = key_blocks(k), key_blocks(v)
    s = jnp.einsum('bgnqhd,bgnkhd->bgnhqk', qs, ks) * (dh ** -0.5)
    qi = jnp.arange(Q)[:, None] + Q
    kj = jnp.arange(2 * Q)[None, :]
    dist = qi - kj
    abs_k = jnp.arange(nb)[:, None, None] * Q + kj[None] - Q
    mask = (dist >= 0) & (dist <= span) & (abs_k >= 0)
    s = jnp.where(mask[None, None, :, None], s, -jnp.inf)
    lse = jax.nn.logsumexp(s, axis=-1)
    p = jnp.exp(s - lse[..., None])
    o = jnp.einsum('bgnhqk,bgnkhd->bgnqhd', p, vs)
    o = o.reshape(B, dilation, nb * Q, H, dh)[:, :, :L].transpose(0, 2, 1, 3, 4).reshape(B, S, H, dh)
    lse = lse.transpose(0, 1, 2, 4, 3).reshape(B, dilation, nb * Q, H)[:, :, :L]
    lse = lse.transpose(0, 2, 1, 3).reshape(B, S, H)
    return o, lse


def dilated_mixer(h, positions, w_in, w_out):
    B, S, _ = h.shape
    proj = (h @ w_in).astype(jnp.float32).reshape(B, S, A_N_GROUPS, 3, A_HEADS, A_HEAD_DIM)
    outs, lses = [], []
    for g, (window, dilation) in enumerate(A_GROUPS):
        q = partial_rope(proj[:, :, g, 0], positions)
        k = partial_rope(proj[:, :, g, 1], positions)
        o, lse = dilated_band_attention(q, k, proj[:, :, g, 2], window, dilation)
        outs.append(o)
        lses.append(lse)
    alpha = jax.nn.softmax(jnp.stack(lses), axis=0)
    o = jnp.sum(alpha[..., None] * jnp.stack(outs), axis=0)
    return o.reshape(B, S, A_WIDTH).astype(h.dtype) @ w_out


def conv_ffn(h, w_up, conv_w, conv_b, w_down):
    u = h @ w_up
    u = lax.conv_general_dilated(u, conv_w[:, None, :], window_strides=(1,),
                                 padding=[(CONV_WIDTH - 1, 0)],
                                 dimension_numbers=('NWC', 'WIO', 'NWC'),
                                 feature_group_count=u.shape[-1]) + conv_b
    gate, val = jnp.split(u, 2, axis=-1)
    return (jax.nn.silu(gate) * val) @ w_down


def setup_inputs(seed: int = 0) -> dict:
    key = jax.random.key(seed)
    ks = jax.random.split(key, 20)

    def nrm(k, shape, scale):
        return jax.random.normal(k, shape, jnp.float32) * scale

    x = nrm(ks[0], (BATCH, SEQ, D_MODEL), 1.0)
    c = nrm(ks[1], (BATCH, D_MODEL), 1.0)
    positions = (jnp.arange(SEQ, dtype=jnp.int32)[None, :]
                 + jax.random.randint(ks[2], (BATCH, 1), 0, 1024, dtype=jnp.int32))
    w_ada = nrm(ks[3], (D_MODEL, DEPTH * N_MOD * D_MODEL), 0.5 * D_MODEL ** -0.5)
    b_ada = nrm(ks[4], (DEPTH * N_MOD * D_MODEL,), 0.02)
    norm_mix = 1.0 + nrm(ks[5], (DEPTH, D_MODEL), 0.05)
    norm_ffn = 1.0 + nrm(ks[6], (DEPTH, D_MODEL), 0.05)
    norm_out = 1.0 + nrm(ks[7], (D_MODEL,), 0.05)
    m_w_in = nrm(ks[8], (N_MLSTM_LAYERS, D_MODEL, M_IN_W), D_MODEL ** -0.5)
    kb1, kb2 = jax.random.split(ks[9])
    i_bias = nrm(kb1, (N_MLSTM_LAYERS, M_HEADS), 0.1)
    f_bias = jnp.linspace(3.0, 6.0, M_HEADS, dtype=jnp.float32)[None, :] + nrm(kb2, (N_MLSTM_LAYERS, M_HEADS), 0.1)
    m_gate_b = jnp.concatenate([i_bias, f_bias], axis=-1)
    m_head_norm = 1.0 + nrm(ks[10], (N_MLSTM_LAYERS, M_V_W), 0.05)
    m_w_out = nrm(ks[11], (N_MLSTM_LAYERS, M_V_W, D_MODEL), M_V_W ** -0.5)
    a_w_in = nrm(ks[12], (N_ATTN_LAYERS, D_MODEL, A_IN_W), D_MODEL ** -0.5)
    a_w_out = nrm(ks[13], (N_ATTN_LAYERS, A_WIDTH, D_MODEL), A_WIDTH ** -0.5)
    f_w_up = nrm(ks[14], (DEPTH, D_MODEL, 2 * D_FF), D_MODEL ** -0.5)
    f_conv_w = nrm(ks[15], (DEPTH, CONV_WIDTH, 2 * D_FF), CONV_WIDTH ** -0.5)
    f_conv_b = nrm(ks[16], (DEPTH, 2 * D_FF), 0.02)
    f_w_down = nrm(ks[17], (DEPTH, D_FF, D_MODEL), D_FF ** -0.5)
    return {"x": x, "c": c, "positions": positions, "w_ada": w_ada, "b_ada": b_ada,
            "norm_mix": norm_mix, "norm_ffn": norm_ffn, "norm_out": norm_out,
            "m_w_in": m_w_in, "m_gate_b": m_gate_b, "m_head_norm": m_head_norm, "m_w_out": m_w_out,
            "a_w_in": a_w_in, "a_w_out": a_w_out,
            "f_w_up": f_w_up, "f_conv_w": f_conv_w, "f_conv_b": f_conv_b, "f_w_down": f_w_down}


def reference(x, c, positions, w_ada, b_ada, norm_mix, norm_ffn, norm_out,
              m_w_in, m_gate_b, m_head_norm, m_w_out, a_w_in, a_w_out,
              f_w_up, f_conv_w, f_conv_b, f_w_down):
    B = x.shape[0]
    mod = (jax.nn.silu(c) @ w_ada + b_ada).reshape(B, DEPTH, N_MOD, 1, D_MODEL)
    for i in range(DEPTH):
        sh_m, sc_m, g_m, sh_f, sc_f, g_f = (mod[:, i, j] for j in range(N_MOD))
        h = rms_norm(x, norm_mix[i]) * (1.0 + sc_m) + sh_m
        j = i // N_MIXERS
        if i % N_MIXERS == 0:
            y = mlstm_mixer(h, m_w_in[j], m_gate_b[j], m_head_norm[j], m_w_out[j])
        else:
            y = dilated_mixer(h, positions, a_w_in[j], a_w_out[j])
        x = x + g_m * y
        h = rms_norm(x, norm_ffn[i]) * (1.0 + sc_f) + sh_f
        x = x + g_f * conv_ffn(h, f_w_up[i], f_conv_w[i], f_conv_b[i], f_w_down[i])
    return rms_norm(x, norm_out)
```

```python
import functools

import jax
import jax.numpy as jnp
from jax import lax
from jax.experimental import pallas as pl
from jax.experimental.pallas import tpu as pltpu

F32 = jnp.float32
BF16 = jnp.bfloat16

LANES = 128
SUBLANES = 8
MIB = 1024 * 1024

N_MOD = 6
M_HEADS = 8
M_CHUNK = 64
M_GATE_CAP = 15.0
A_HEAD_DIM = 128
A_GROUPS = ((128, 1), (512, 4), (2048, 16))
A_BLOCK = 128
ROPE_THETA = 500000.0
ROPE_DIM = A_HEAD_DIM // 4
ROPE_HALF = ROPE_DIM // 2
EPS = 1e-6

NT_DIMS = (((1,), (1,)), ((), ()))
TN_DIMS = (((0,), (0,)), ((), ()))


def _tile(n, pref):
    if n <= pref:
        return n
    t = pref - pref % LANES
    while n % t:
        t -= LANES
    return t


def _params(n_axes, vmem_mib):
    return pltpu.CompilerParams(dimension_semantics=("arbitrary",) * n_axes,
                                vmem_limit_bytes=vmem_mib * MIB)


def _norm_mod(x, g, sc, sh):
    ms = jnp.mean(x * x, axis=-1, keepdims=True)
    return (x * lax.rsqrt(ms + EPS) * g) * (1.0 + sc) + sh


def _ada_kernel(c_ref, w_ref, b_ref, o_ref):
    c = c_ref[...]
    a = (c * jax.nn.sigmoid(c)).astype(BF16)
    o_ref[...] = jnp.dot(a, w_ref[...].astype(BF16), preferred_element_type=F32) + b_ref[...]


def _ada(c, w_ada, b_ada):
    B, D = c.shape
    N = w_ada.shape[1]
    tn = _tile(N, 1024)
    return pl.pallas_call(
        _ada_kernel,
        grid=(N // tn,),
        in_specs=[pl.BlockSpec((B, D), lambda j: (0, 0)),
                  pl.BlockSpec((D, tn), lambda j: (0, j)),
                  pl.BlockSpec((1, tn), lambda j: (0, j))],
        out_specs=pl.BlockSpec((B, tn), lambda j: (0, j)),
        out_shape=jax.ShapeDtypeStruct((B, N), F32),
        compiler_params=_params(1, 40),
        name="ada_mod",
    )(c, w_ada, b_ada.reshape(1, N))


def _rope_kernel(pos_ref, invf_ref, cos_ref, sin_ref):
    ang = pos_ref[...].astype(F32) * invf_ref[...]
    lane = lax.broadcasted_iota(jnp.int32, ang.shape, 1)
    c, s = jnp.cos(ang), jnp.sin(ang)
    cos_ref[...] = jnp.where(lane < ROPE_DIM, c, 1.0)
    sin_ref[...] = jnp.where(lane < ROPE_HALF, -s, jnp.where(lane < ROPE_DIM, s, 0.0))


def _rope_tables(positions):
    T = positions.size
    tm = _tile(T, 2048)
    inv_freq = ROPE_THETA ** (-jnp.arange(ROPE_HALF, dtype=F32) / ROPE_HALF)
    invf = jnp.tile(inv_freq, LANES // ROPE_HALF).reshape(1, LANES)
    return pl.pallas_call(
        _rope_kernel,
        grid=(T // tm,),
        in_specs=[pl.BlockSpec((tm, 1), lambda i: (i, 0)),
                  pl.BlockSpec((1, LANES), lambda i: (0, 0))],
        out_specs=[pl.BlockSpec((tm, LANES), lambda i: (i, 0))] * 2,
        out_shape=[jax.ShapeDtypeStruct((T, LANES), F32)] * 2,
        compiler_params=_params(1, 32),
        name="rope_tables",
    )(positions.reshape(T, 1), invf)


def _inproj_kernel(*refs, has_gate, rope_slab):
    x_ref, g_ref, sc_ref, sh_ref, w_ref, cs_ref = refs[:6]
    refs = refs[6:]
    if has_gate:
        wg_ref, refs = refs[0], refs[1:]
    if rope_slab:
        cos_ref, sin_ref, refs = refs[0], refs[1], refs[2:]
    o_ref, refs = refs[0], refs[1:]
    if has_gate:
        og_ref, refs = refs[0], refs[1:]
    h_ref = refs[0]
    j = pl.program_id(1)

    @pl.when(j == 0)
    def _():
        h = _norm_mod(x_ref[...], g_ref[...], sc_ref[...], sh_ref[...]).astype(BF16)
        h_ref[...] = h
        if has_gate:
            og_ref[...] = jnp.dot(h, wg_ref[...], preferred_element_type=F32)

    acc = jnp.dot(h_ref[...], w_ref[...], preferred_element_type=F32) * cs_ref[...]
    if not rope_slab:
        o_ref[...] = acc.astype(o_ref.dtype)
        return

    tn = acc.shape[1]
    slab_kind = (j // (rope_slab // tn)) % 3

    @pl.when(slab_kind < 2)
    def _():
        cos, sin = cos_ref[...], sin_ref[...]
        lane = lax.broadcasted_iota(jnp.int32, cos.shape, 1)
        for hd in range(tn // A_HEAD_DIM):
            cols = slice(hd * A_HEAD_DIM, (hd + 1) * A_HEAD_DIM)
            t = acc[:, cols]
            partner = jnp.where(lane < ROPE_HALF,
                                pltpu.roll(t, A_HEAD_DIM - ROPE_HALF, 1),
                                pltpu.roll(t, ROPE_HALF, 1))
            o_ref[:, cols] = (t * cos + partner * sin).astype(o_ref.dtype)

    @pl.when(slab_kind == 2)
    def _():
        o_ref[...] = acc.astype(o_ref.dtype)


def _inproj(x, norm_g, mod4, sc_idx, sh_idx, w, col_scale, seq, *, w_gate=None, rope=None, rope_slab=0):
    T, D = x.shape
    N = w.shape[1]
    tm = _tile(seq, 1024)
    tn = _tile(rope_slab, 1024) if rope_slab else _tile(N, 1024)
    per_seq = seq // tm
    in_specs = [pl.BlockSpec((tm, D), lambda i, j: (i, 0)),
                pl.BlockSpec((1, D), lambda i, j: (0, 0)),
                pl.BlockSpec((None, None, 1, D), lambda i, j: (i // per_seq, sc_idx, 0, 0)),
                pl.BlockSpec((None, None, 1, D), lambda i, j: (i // per_seq, sh_idx, 0, 0)),
                pl.BlockSpec((D, tn), lambda i, j: (0, j)),
                pl.BlockSpec((1, tn), lambda i, j: (0, j))]
    args = [x, norm_g.reshape(1, D), mod4, mod4, w, col_scale.reshape(1, N)]
    out_specs = [pl.BlockSpec((tm, tn), lambda i, j: (i, j))]
    out_shape = [jax.ShapeDtypeStruct((T, N), BF16)]
    if w_gate is not None:
        in_specs.append(pl.BlockSpec((D, LANES), lambda i, j: (0, 0)))
        args.append(w_gate)
        out_specs.append(pl.BlockSpec((tm, LANES), lambda i, j: (i, 0)))
        out_shape.append(jax.ShapeDtypeStruct((T, LANES), F32))
    if rope is not None:
        in_specs += [pl.BlockSpec((tm, LANES), lambda i, j: (i, 0))] * 2
        args += list(rope)
    return pl.pallas_call(
        functools.partial(_inproj_kernel, has_gate=w_gate is not None, rope_slab=rope_slab),
        grid=(T // tm, N // tn),
        in_specs=in_specs,
        out_specs=out_specs,
        out_shape=out_shape,
        scratch_shapes=[pltpu.VMEM((tm, D), BF16)],
        compiler_params=_params(2, 56),
        name="inproj_rope" if rope_slab else "inproj",
    )(*args)


def _mlstm_kernel(gb_ref, q_ref, k_ref, v_ref, o_ref, ig_ref, fg_ref, hg_ref, y_ref,
                  ct_s, n_s, m_s, b_s, li_s, *, n_heads):
    hd = pl.program_id(1)
    NC, L = ig_ref.shape
    li = M_GATE_CAP * jnp.tanh((ig_ref[...] + gb_ref[hd]) / M_GATE_CAP)
    fz = M_GATE_CAP * jnp.tanh((fg_ref[...] + gb_ref[n_heads + hd]) / M_GATE_CAP)
    lf = jnp.minimum(fz, 0.0) - jnp.log1p(jnp.exp(-jnp.abs(fz)))
    r = lax.broadcasted_iota(jnp.int32, (L, L), 0)
    c = lax.broadcasted_iota(jnp.int32, (L, L), 1)
    b_s[...] = jnp.dot(lf, (r <= c).astype(F32), precision=lax.Precision.HIGHEST,
                       preferred_element_type=F32)
    li_s[...] = li
    ct_s[...] = jnp.zeros_like(ct_s)
    n_s[...] = jnp.zeros_like(n_s)
    m_s[...] = jnp.zeros_like(m_s)
    eye = r == c
    causal = c <= r
    hg = hg_ref[...]

    def to_col(row):
        return jnp.sum(jnp.where(eye, row, 0.0), axis=1, keepdims=True)

    def chunk(ci, carry):
        rows = pl.ds(pl.multiple_of(ci * L, L), L)
        q, k, v = q_ref[rows, :], k_ref[rows, :], v_ref[rows, :]
        b_row = b_s[pl.ds(ci, 1), :]
        i_row = li_s[pl.ds(ci, 1), :]
        b_col = to_col(b_row)
        m_prev = m_s[...]
        dmat = jnp.where(causal, b_col - b_row + i_row, -jnp.inf)
        inter = b_col + m_prev
        m_row = jnp.maximum(inter, jnp.max(dmat, axis=1, keepdims=True))
        a_inter = jnp.exp(inter - m_row)
        w = lax.dot_general(q, k, NT_DIMS, preferred_element_type=F32) * jnp.exp(dmat - m_row)
        ct = ct_s[...]
        num = (jnp.dot(w.astype(BF16), v, preferred_element_type=F32)
               + a_inter * jnp.dot(q, ct.astype(BF16), preferred_element_type=F32))
        den = (jnp.sum(w, axis=1, keepdims=True)
               + a_inter * jnp.sum(q.astype(F32) * n_s[...], axis=1, keepdims=True))
        hc = num / jnp.maximum(jnp.abs(den), jnp.exp(-m_row))
        ht = hc * lax.rsqrt(jnp.mean(hc * hc, axis=1, keepdims=True) + EPS) * hg
        y_ref[rows, :] = (jax.nn.sigmoid(o_ref[rows, :].astype(F32)) * ht).astype(y_ref.dtype)
        b_last = b_row[:, L - 1:L]
        g_row = b_last - b_row + i_row
        m_new = jnp.maximum(b_last + m_prev, jnp.max(g_row, axis=1, keepdims=True))
        wk_col = to_col(jnp.exp(g_row - m_new))
        decay = jnp.exp(b_last + m_prev - m_new)
        wv = (wk_col * v.astype(F32)).astype(BF16)
        ct_s[...] = decay * ct + lax.dot_general(k, wv, TN_DIMS, preferred_element_type=F32)
        n_s[...] = decay * n_s[...] + jnp.sum(wk_col * k.astype(F32), axis=0, keepdims=True)
        m_s[...] = m_new
        return carry

    lax.fori_loop(0, NC, chunk, 0)


def _mlstm(proj, gates, gate_b, head_g, batch, seq, n_heads, dqk, dv):
    T = proj.shape[0]
    NC = seq // M_CHUNK
    g = gates[:, :2 * n_heads].reshape(batch, NC, M_CHUNK, 2 * n_heads).transpose(0, 3, 1, 2)
    ig, fg = g[:, :n_heads], g[:, n_heads:]
    qk_w, v_w = n_heads * dqk, n_heads * dv
    k_blk, v_blk, o_blk = qk_w // dqk, 2 * qk_w // dv, (2 * qk_w + v_w) // dv
    gate_spec = pl.BlockSpec((None, None, NC, M_CHUNK), lambda b, h: (b, h, 0, 0))
    return pl.pallas_call(
        functools.partial(_mlstm_kernel, n_heads=n_heads),
        grid=(batch, n_heads),
        in_specs=[pl.BlockSpec(memory_space=pltpu.SMEM),
                  pl.BlockSpec((seq, dqk), lambda b, h: (b, h)),
                  pl.BlockSpec((seq, dqk), lambda b, h: (b, k_blk + h)),
                  pl.BlockSpec((seq, dv), lambda b, h: (b, v_blk + h)),
                  pl.BlockSpec((seq, dv), lambda b, h: (b, o_blk + h)),
                  gate_spec, gate_spec,
                  pl.BlockSpec((1, dv), lambda b, h: (0, h))],
        out_specs=pl.BlockSpec((seq, dv), lambda b, h: (b, h)),
        out_shape=jax.ShapeDtypeStruct((T, v_w), BF16),
        scratch_shapes=[pltpu.VMEM((dqk, dv), F32), pltpu.VMEM((1, dqk), F32), pltpu.VMEM((1, 1), F32),
                        pltpu.VMEM((NC, M_CHUNK), F32), pltpu.VMEM((NC, M_CHUNK), F32)],
        compiler_params=_params(2, 32),
        name="mlstm",
    )(gate_b, proj, proj, proj, proj, ig, fg, head_g.reshape(1, v_w))


def _attn_kernel(q_ref, k_ref, v_ref, o_ref, lse_ref, *, n_blocks, heads_per_step):
    hb = pl.program_id(2)
    Q = A_BLOCK

    @pl.when(hb == 0)
    def _():
        lse_ref[...] = jnp.zeros_like(lse_ref)

    qi = lax.broadcasted_iota(jnp.int32, (Q, 2 * Q), 0)
    kj = lax.broadcasted_iota(jnp.int32, (Q, 2 * Q), 1)
    band = (kj >= qi) & (kj <= qi + Q)
    tri = (lax.broadcasted_iota(jnp.int32, (Q, Q), 1)
           <= lax.broadcasted_iota(jnp.int32, (Q, Q), 0))
    lane = lax.broadcasted_iota(jnp.int32, (Q, LANES), 1)

    def one_block(q, k, v, mask, lse_old, lse_lane):
        s = jnp.where(mask, lax.dot_general(q, k, NT_DIMS, preferred_element_type=F32), -jnp.inf)
        m = jnp.max(s, axis=1, keepdims=True)
        p = jnp.exp(s - m)
        l = jnp.sum(p, axis=1, keepdims=True)
        o = jnp.dot(p.astype(BF16), v, preferred_element_type=F32) / l
        return o, jnp.where(lane == lse_lane, m + jnp.log(l), lse_old)

    for hd in range(heads_per_step):
        cols = slice(hd * A_HEAD_DIM, (hd + 1) * A_HEAD_DIM)
        lse_lane = hb * heads_per_step + hd
        o, lse = one_block(q_ref[0:Q, cols], k_ref[0:Q, cols], v_ref[0:Q, cols], tri,
                           lse_ref[0:Q, :], lse_lane)
        o_ref[0:Q, cols] = o.astype(o_ref.dtype)
        lse_ref[0:Q, :] = lse

        def block(n, carry, cols=cols, lse_lane=lse_lane):
            rows = pl.ds(pl.multiple_of(n * Q, Q), Q)
            kv_rows = pl.ds(pl.multiple_of((n - 1) * Q, Q), 2 * Q)
            o, lse = one_block(q_ref[rows, cols], k_ref[kv_rows, cols], v_ref[kv_rows, cols], band,
                               lse_ref[rows, :], lse_lane)
            o_ref[rows, cols] = o.astype(o_ref.dtype)
            lse_ref[rows, :] = lse
            return carry

        lax.fori_loop(1, n_blocks, block, 0)


def _attention_group(proj, group, dilation, batch, seq, width):
    T, W3 = proj.shape
    L = seq // dilation
    hw = _tile(width, 1024)
    n_hb = width // hw
    cb = W3 // hw
    base = group * 3 * n_hb

    def in_spec(kind):
        return pl.BlockSpec((None, L, hw), lambda b, r, hb: (b, 0, r * cb + base + kind * n_hb + hb))

    o, lse = pl.pallas_call(
        functools.partial(_attn_kernel, n_blocks=L // A_BLOCK, heads_per_step=hw // A_HEAD_DIM),
        grid=(batch, dilation, n_hb),
        in_specs=[in_spec(0), in_spec(1), in_spec(2)],
        out_specs=[pl.BlockSpec((None, L, hw), lambda b, r, hb: (b, 0, r * n_hb + hb)),
                   pl.BlockSpec((None, L, LANES), lambda b, r, hb: (b, 0, r))],
        out_shape=[jax.ShapeDtypeStruct((batch, L, dilation * width), BF16),
                   jax.ShapeDtypeStruct((batch, L, dilation * LANES), F32)],
        compiler_params=_params(3, 56),
        name=f"attn_d{dilation}",
    )(*[proj.reshape(batch, L, dilation * W3)] * 3)
    return o.reshape(T, width), lse.reshape(T, LANES)


def _outproj_kernel(*refs, n_mix):
    if n_mix:
        o_refs, l_refs, refs = refs[:n_mix], refs[n_mix:2 * n_mix], refs[2 * n_mix:]
        w_ref, x_ref, gm_ref, out_ref, y_ref = refs

        @pl.when(pl.program_id(1) == 0)
        def _():
            ls = [l[...] for l in l_refs]
            mx = functools.reduce(jnp.maximum, ls)
            es = [jnp.exp(l - mx) for l in ls]
            inv = 1.0 / functools.reduce(jnp.add, es)
            alphas = [e * inv for e in es]
            for hd in range(y_ref.shape[1] // A_HEAD_DIM):
                cols = slice(hd * A_HEAD_DIM, (hd + 1) * A_HEAD_DIM)
                y = functools.reduce(jnp.add, [a[:, hd:hd + 1] * o[:, cols].astype(F32)
                                               for a, o in zip(alphas, o_refs)])
                y_ref[:, cols] = y.astype(y_ref.dtype)
    else:
        y_ref, w_ref, x_ref, gm_ref, out_ref = refs
    out_ref[...] = x_ref[...] + gm_ref[...] * jnp.dot(y_ref[...], w_ref[...], preferred_element_type=F32)


def _outproj(ys, lses, w, x, mod4, g_idx, seq):
    T, D = x.shape
    K = w.shape[0]
    n_mix = len(ys) if lses else 0
    tm = _tile(seq, 512 if n_mix else 1024)
    tn = _tile(D, 1024)
    per_seq = seq // tm
    row_spec = pl.BlockSpec((tm, K), lambda i, j: (i, 0))
    in_specs = [row_spec] * len(ys) + [pl.BlockSpec((tm, LANES), lambda i, j: (i, 0))] * n_mix + [
        pl.BlockSpec((K, tn), lambda i, j: (0, j)),
        pl.BlockSpec((tm, tn), lambda i, j: (i, j)),
        pl.BlockSpec((None, None, 1, tn), lambda i, j: (i // per_seq, g_idx, 0, j))]
    return pl.pallas_call(
        functools.partial(_outproj_kernel, n_mix=n_mix),
        grid=(T // tm, D // tn),
        in_specs=in_specs,
        out_specs=pl.BlockSpec((tm, tn), lambda i, j: (i, j)),
        out_shape=jax.ShapeDtypeStruct((T, D), F32),
        scratch_shapes=[pltpu.VMEM((tm, K), BF16)] if n_mix else [],
        compiler_params=_params(2, 56),
        name="outproj_mix" if n_mix else "outproj",
    )(*ys, *lses, w, x, mod4)


def _ffn_kernel(x_ref, g_ref, sc_ref, sh_ref, gate_ref, wg_ref, wv_ref, cwg_ref, cwv_ref, cbg_ref, cbv_ref,
                wd_ref, go_ref, out_ref, h_ref, carry_g, carry_v, *, tiles_per_seq, final_norm):
    i, j = pl.program_id(0), pl.program_id(1)
    tm = x_ref.shape[0]

    @pl.when(j == 0)
    def _():
        h_ref[...] = _norm_mod(x_ref[...], g_ref[...], sc_ref[...], sh_ref[...]).astype(BF16)

    @pl.when((i == 0) & (j == 0))
    def _():
        carry_g[...] = jnp.zeros_like(carry_g)
        carry_v[...] = jnp.zeros_like(carry_v)

    h = h_ref[...]
    seq_start = i % tiles_per_seq == 0
    row = lax.broadcasted_iota(jnp.int32, (tm, 1), 0)

    def causal_conv(u, carry, cw_ref, cb_ref):
        prev = jnp.where(seq_start, 0.0, carry[j])
        carry[j] = u[tm - SUBLANES:, :]
        u1 = jnp.where(row == 0, prev[SUBLANES - 1:, :], pltpu.roll(u, 1, 0))
        u2 = jnp.where(row == 0, prev[SUBLANES - 2:SUBLANES - 1, :],
                       jnp.where(row == 1, prev[SUBLANES - 1:, :], pltpu.roll(u, 2, 0)))
        cw = cw_ref[...]
        return cw[2:3, :] * u + cw[1:2, :] * u1 + cw[0:1, :] * u2 + cb_ref[...]

    ug = causal_conv(jnp.dot(h, wg_ref[...], preferred_element_type=F32), carry_g, cwg_ref, cbg_ref)
    uv = causal_conv(jnp.dot(h, wv_ref[...], preferred_element_type=F32), carry_v, cwv_ref, cbv_ref)
    act = (ug * jax.nn.sigmoid(ug) * uv).astype(BF16)
    contrib = jnp.dot(act, wd_ref[...], preferred_element_type=F32)

    @pl.when(j == 0)
    def _():
        out_ref[...] = contrib

    @pl.when(j > 0)
    def _():
        out_ref[...] += contrib

    @pl.when(j == pl.num_programs(1) - 1)
    def _():
        xn = x_ref[...] + gate_ref[...] * out_ref[...]
        if final_norm:
            ms = jnp.mean(xn * xn, axis=-1, keepdims=True)
            xn = xn * lax.rsqrt(ms + EPS) * go_ref[...]
        out_ref[...] = xn


def _ffn(x, norm_g, mod4, sh_idx, sc_idx, g_idx, w_up, conv_w, conv_b, w_down, norm_out, seq, final_norm):
    T, D = x.shape
    F = w_down.shape[0]
    tm = _tile(seq, 512)
    cw = _tile(F, 512)
    nj = F // cw
    per_seq = seq // tm
    mod_spec = lambda idx: pl.BlockSpec((None, None, 1, D), lambda i, j: (i // per_seq, idx, 0, 0))
    conv_b = conv_b.reshape(1, 2 * F)
    return pl.pallas_call(
        functools.partial(_ffn_kernel, tiles_per_seq=per_seq, final_norm=final_norm),
        grid=(T // tm, nj),
        in_specs=[pl.BlockSpec((tm, D), lambda i, j: (i, 0)),
                  pl.BlockSpec((1, D), lambda i, j: (0, 0)),
                  mod_spec(sc_idx), mod_spec(sh_idx), mod_spec(g_idx),
                  pl.BlockSpec((D, cw), lambda i, j: (0, j)),
                  pl.BlockSpec((D, cw), lambda i, j: (0, nj + j)),
                  pl.BlockSpec((3, cw), lambda i, j: (0, j)),
                  pl.BlockSpec((3, cw), lambda i, j: (0, nj + j)),
                  pl.BlockSpec((1, cw), lambda i, j: (0, j)),
                  pl.BlockSpec((1, cw), lambda i, j: (0, nj + j)),
                  pl.BlockSpec((cw, D), lambda i, j: (j, 0)),
                  pl.BlockSpec((1, D), lambda i, j: (0, 0))],
        out_specs=pl.BlockSpec((tm, D), lambda i, j: (i, 0)),
        out_shape=jax.ShapeDtypeStruct((T, D), F32),
        scratch_shapes=[pltpu.VMEM((tm, D), BF16),
                        pltpu.VMEM((nj, SUBLANES, cw), F32),
                        pltpu.VMEM((nj, SUBLANES, cw), F32)],
        compiler_params=_params(2, 56),
        name="conv_ffn",
    )(x, norm_g.reshape(1, D), mod4, mod4, mod4, w_up, w_up, conv_w, conv_w, conv_b, conv_b,
      w_down, norm_out.reshape(1, D))


def kernel(x, c, positions, w_ada, b_ada, norm_mix, norm_ffn, norm_out, m_w_in, m_gate_b, m_head_norm,
           m_w_out, a_w_in, a_w_out, f_w_up, f_conv_w, f_conv_b, f_w_down):
    B, S, D = x.shape
    depth = norm_mix.shape[0]
    T = B * S
    mod4 = _ada(c, w_ada, b_ada).reshape(B, depth * N_MOD, 1, D)
    xt = x.reshape(T, D)
    rope = None
    for i in range(depth):
        m0 = i * N_MOD
        jl = i // 2
        if i % 2 == 0:
            n_heads = m_gate_b.shape[1] // 2
            v_w = m_w_out.shape[1]
            dv = v_w // n_heads
            dqk = dv // 2
            qk_w = n_heads * dqk
            n_main = 2 * qk_w + 2 * v_w
            w_in = m_w_in[jl]
            col_scale = jnp.concatenate([jnp.full((qk_w,), dqk ** -0.5, F32), jnp.ones((n_main - qk_w,), F32)])
            w_gate = jnp.pad(w_in[:, n_main:], ((0, 0), (0, LANES - 2 * n_heads))).astype(BF16)
            proj, gates = _inproj(xt, norm_mix[i], mod4, m0 + 1, m0, w_in[:, :n_main].astype(BF16), col_scale, S,
                                  w_gate=w_gate)
            y = _mlstm(proj, gates, m_gate_b[jl], m_head_norm[jl], B, S, n_heads, dqk, dv)
            xt = _outproj([y], [], m_w_out[jl].astype(BF16), xt, mod4, m0 + 2, S)
        else:
            width = a_w_out.shape[1]
            if rope is None:
                rope = _rope_tables(positions)
            n_groups = len(A_GROUPS)
            slab_scale = jnp.concatenate([jnp.full((width,), A_HEAD_DIM ** -0.5, F32), jnp.ones((2 * width,), F32)])
            proj = _inproj(xt, norm_mix[i], mod4, m0 + 1, m0, a_w_in[jl].astype(BF16),
                           jnp.tile(slab_scale, n_groups), S, rope=rope, rope_slab=width)[0]
            outs = [_attention_group(proj, g, d, B, S, width) for g, (_, d) in enumerate(A_GROUPS)]
            xt = _outproj([o for o, _ in outs], [l for _, l in outs], a_w_out[jl].astype(BF16), xt, mod4, m0 + 2, S)
        xt = _ffn(xt, norm_ffn[i], mod4, m0 + 3, m0 + 4, m0 + 5, f_w_up[i].astype(BF16), f_conv_w[i], f_conv_b[i],
                  f_w_down[i].astype(BF16), norm_out, S, final_norm=(i == depth - 1))
    return xt.reshape(B, S, D)
```
